```python
import jax
import jax.numpy as jnp
from jax import lax
import numpy as np

D_MODEL = 1024
BATCH = 1
SEQ = 16384
DEPTH = 4

GRID_W = 64
CTX_LEN = 256
HEAD_DIM = 64
Q_BLOCK = 128
ROPE_BASE = 10000.0
EPS = 1e-6
NEG_INF = -1e30
N_MOD = 9
FFN_RES = 0.5
D_FF = 2816

MLA_HEADS = 8
MLA_Q_RANK = 256
MLA_KV_RANK = 128
MLA_NOPE = 64
MLA_ROPE = 32
MLA_V = 64
GQA_HEADS = 8
GQA_KV_HEADS = 2
SWA_HEADS = 8
SWA_KV_HEADS = 2
WINDOW = 128
NA_HEADS = 8
NA_ROWS = 8
NA_COLS = 16

N_AB = (DEPTH + 1) // 2
N_CD = DEPTH // 2
AB_IN = MLA_Q_RANK + MLA_KV_RANK + MLA_ROPE + (GQA_HEADS + 2 * GQA_KV_HEADS) * HEAD_DIM
AB_OUT = MLA_HEADS * MLA_V + GQA_HEADS * HEAD_DIM
CD_IN = (SWA_HEADS + 2 * SWA_KV_HEADS) * HEAD_DIM + 3 * NA_HEADS * HEAD_DIM
CD_OUT = (SWA_HEADS + NA_HEADS) * HEAD_DIM

kernel_name = 'hybrid_mla_gqa_swa_na_macaron_dit'


def rms_norm(x, g):
    xf = x.astype(jnp.float32)
    y = xf * lax.rsqrt(jnp.mean(xf * xf, axis=-1, keepdims=True) + EPS)
    return y.astype(x.dtype) * g


def ada_in(x, mod, slot, g):
    return rms_norm(x, g) * (1 + mod[:, 3 * slot + 1, None, :]) + mod[:, 3 * slot, None, :]


def ada_out(x, y, mod, slot, g, w):
    return x + w * mod[:, 3 * slot + 2, None, :] * rms_norm(y, g)


def swiglu(h, wg, wu, wd):
    return (jax.nn.silu(h @ wg) * (h @ wu)) @ wd


def split_cols(t, sizes):
    return jnp.split(t, np.cumsum(sizes)[:-1].tolist(), axis=-1)


def axial_rope_tables(n_tok, rot_dim, dtype):
    t = jnp.arange(n_tok, dtype=jnp.int32)
    row = (t // GRID_W).astype(jnp.float32)
    col = (t % GRID_W).astype(jnp.float32)
    per_axis = rot_dim // 2
    inv = ROPE_BASE ** (-jnp.arange(0, per_axis, 2, dtype=jnp.float32) / per_axis)
    ang = jnp.concatenate([row[:, None] * inv, col[:, None] * inv], axis=-1)
    return jnp.cos(ang).astype(dtype), jnp.sin(ang).astype(dtype)


def apply_rope(x, cos, sin):
    x1, x2 = jnp.split(x, 2, axis=-1)
    cs, sn = cos[None, :, None, :], sin[None, :, None, :]
    return jnp.concatenate([x1 * cs - x2 * sn, x1 * sn + x2 * cs], axis=-1)


def neighbourhood_tables(rows):
    kr = min(NA_ROWS, rows)
    n_tok = rows * GRID_W
    t = jnp.arange(n_tok, dtype=jnp.int32)
    r, col = t // GRID_W, t % GRID_W
    r0 = jnp.clip(r - kr // 2, 0, rows - kr)
    c0 = jnp.clip(col - NA_COLS // 2, 0, GRID_W - NA_COLS)
    key_r = r0[:, None, None] + jnp.arange(kr, dtype=jnp.int32)[None, :, None]
    key_c = c0[:, None, None] + jnp.arange(NA_COLS, dtype=jnp.int32)[None, None, :]
    idx = (key_r * GRID_W + key_c).reshape(n_tok, kr * NA_COLS)
    rel = ((key_r - r[:, None, None] + NA_ROWS - 1) * (2 * NA_COLS - 1)
           + (key_c - col[:, None, None] + NA_COLS - 1)).reshape(n_tok, kr * NA_COLS)
    return idx, rel


def softmax_with_sink(logits, sink):
    if sink is None:
        return jax.nn.softmax(logits, axis=-1)
    s = jnp.broadcast_to(sink.astype(jnp.float32), logits.shape[:-1] + (1,))
    return jax.nn.softmax(jnp.concatenate([logits, s], axis=-1), axis=-1)[..., :-1]


def dense_gqa(q, k, v, sink=None):
    B, Q, H, d = q.shape
    Hk = k.shape[2]
    qg = q.reshape(B, Q, Hk, H // Hk, d)
    logits = jnp.einsum('bqhgd,bkhd->bhgqk', qg, k).astype(jnp.float32) * (d ** -0.5)
    p = softmax_with_sink(logits, sink).astype(v.dtype)
    o = jnp.einsum('bhgqk,bkhd->bqhgd', p, v)
    return o.reshape(B, Q, H, v.shape[-1])


def to_blocks(q):
    B, S, H, d = q.shape
    return q.reshape(B, S // Q_BLOCK, Q_BLOCK, H, d).transpose(1, 0, 2, 3, 4)


def from_blocks(o):
    nb, B, QB, H, d = o.shape
    return o.transpose(1, 0, 2, 3, 4).reshape(B, nb * QB, H, d)


def blocked_dense(q, k, v):
    return from_blocks(lax.map(lambda qi: dense_gqa(qi, k, v), to_blocks(q)))


def windowed_gqa(q, k, v, kc, vc, sink):
    B, S, H, d = q.shape
    Hk = k.shape[2]
    G = H // Hk
    nc = kc.shape[1]
    span = Q_BLOCK + 2 * WINDOW
    pad = ((0, 0), (WINDOW, WINDOW), (0, 0), (0, 0))
    kp, vp = jnp.pad(k, pad), jnp.pad(v, pad)
    scale = d ** -0.5

    def block(args):
        qi, b = args
        start = b * Q_BLOCK
        kw = lax.dynamic_slice_in_dim(kp, start, span, axis=1)
        vw = lax.dynamic_slice_in_dim(vp, start, span, axis=1)
        qpos = start + jnp.arange(Q_BLOCK, dtype=jnp.int32)
        kpos = start - WINDOW + jnp.arange(span, dtype=jnp.int32)
        valid = ((jnp.abs(qpos[:, None] - kpos[None, :]) <= WINDOW)
                 & (kpos[None, :] >= 0) & (kpos[None, :] < S))
        qg = qi.reshape(B, Q_BLOCK, Hk, G, d)
        lc = jnp.einsum('bqhgd,bkhd->bhgqk', qg, kc).astype(jnp.float32) * scale
        lw = jnp.einsum('bqhgd,bkhd->bhgqk', qg, kw).astype(jnp.float32) * scale
        lw = jnp.where(valid, lw, NEG_INF)
        p = softmax_with_sink(jnp.concatenate([lc, lw], axis=-1), sink).astype(v.dtype)
        o = (jnp.einsum('bhgqk,bkhd->bqhgd', p[..., :nc], vc)
             + jnp.einsum('bhgqk,bkhd->bqhgd', p[..., nc:], vw))
        return o.reshape(B, Q_BLOCK, H, v.shape[-1])

    nb = S // Q_BLOCK
    return from_blocks(lax.map(block, (to_blocks(q), jnp.arange(nb, dtype=jnp.int32))))


def neighbourhood_attend(q, k, v, kc, vc, rpb, nbr_idx, rel_idx):
    B, S, H, d = q.shape
    nb = S // Q_BLOCK
    nk = nbr_idx.shape[1]
    nc = kc.shape[1]
    rpb_flat = rpb.reshape(H, -1)
    scale = d ** -0.5

    def block(args):
        qi, idx, rel = args
        kn, vn = k[:, idx], v[:, idx]
        lc = jnp.einsum('bqhd,bkhd->bhqk', qi, kc).astype(jnp.float32) * scale
        ln = (jnp.einsum('bqhd,bqkhd->bhqk', qi, kn).astype(jnp.float32) * scale
              + rpb_flat[:, rel].astype(jnp.float32)[None])
        p = jax.nn.softmax(jnp.concatenate([lc, ln], axis=-1), axis=-1).astype(v.dtype)
        return (jnp.einsum('bhqk,bkhd->bqhd', p[..., :nc], vc)
                + jnp.einsum('bhqk,bqkhd->bqhd', p[..., nc:], vn))

    xs = (to_blocks(q), nbr_idx.reshape(nb, Q_BLOCK, nk), rel_idx.reshape(nb, Q_BLOCK, nk))
    return from_blocks(lax.map(block, xs))


def ab_mixer(h, hc, w_in, q_norm, w_uq, kv_norm, w_ukv, qn_b, kn_b, w_out, rope_a, rope_h, with_ctx):
    sizes = [MLA_Q_RANK, MLA_KV_RANK, MLA_ROPE, GQA_HEADS * HEAD_DIM,
             GQA_KV_HEADS * HEAD_DIM, GQA_KV_HEADS * HEAD_DIM]

    def project(t, positional):
        Bt, N, _ = t.shape
        cq, ckv, kr, bq, bk, bv = split_cols(t @ w_in, sizes)
        qa = (rms_norm(cq, q_norm) @ w_uq).reshape(Bt, N, MLA_HEADS, MLA_NOPE + MLA_ROPE)
        q_nope, q_rope = jnp.split(qa, [MLA_NOPE], axis=-1)
        kva = (rms_norm(ckv, kv_norm) @ w_ukv).reshape(Bt, N, MLA_HEADS, MLA_NOPE + MLA_V)
        k_nope, va = jnp.split(kva, [MLA_NOPE], axis=-1)
        kr = kr[:, :, None, :]
        bq = rms_norm(bq.reshape(Bt, N, GQA_HEADS, HEAD_DIM), qn_b)
        bk = rms_norm(bk.reshape(Bt, N, GQA_KV_HEADS, HEAD_DIM), kn_b)
        bv = bv.reshape(Bt, N, GQA_KV_HEADS, HEAD_DIM)
        if positional:
            q_rope = apply_rope(q_rope, *rope_a)
            kr = apply_rope(kr, *rope_a)
            bq = apply_rope(bq, *rope_h)
            bk = apply_rope(bk, *rope_h)
        qa = jnp.concatenate([q_nope, q_rope], axis=-1)
        ka = jnp.concatenate([k_nope, jnp.broadcast_to(kr, (Bt, N, MLA_HEADS, MLA_ROPE))], axis=-1)
        return qa, ka, va, bq, bk, bv

    B, S, _ = h.shape
    qa, ka, va, bq, bk, bv = project(h, True)
    qa_c, ka_c, va_c, bq_c, bk_c, bv_c = project(hc, False)
    o_a = blocked_dense(qa, jnp.concatenate([ka_c, ka], axis=1), jnp.concatenate([va_c, va], axis=1))
    o_b = blocked_dense(bq, jnp.concatenate([bk_c, bk], axis=1), jnp.concatenate([bv_c, bv], axis=1))
    y = jnp.concatenate([o_a.reshape(B, S, -1), o_b.reshape(B, S, -1)], axis=-1) @ w_out
    yc = None
    if with_ctx:
        Bc, N = hc.shape[:2]
        oc_a = dense_gqa(qa_c, ka_c, va_c)
        oc_b = dense_gqa(bq_c, bk_c, bv_c)
        yc = jnp.concatenate([oc_a.reshape(Bc, N, -1), oc_b.reshape(Bc, N, -1)], axis=-1) @ w_out
    return y, yc


def cd_mixer(h, hc, w_in, sink, rpb, w_out, rope_h, nbr_idx, rel_idx, with_ctx):
    sizes = [SWA_HEADS * HEAD_DIM, SWA_KV_HEADS * HEAD_DIM, SWA_KV_HEADS * HEAD_DIM,
             NA_HEADS * HEAD_DIM, NA_HEADS * HEAD_DIM, NA_HEADS * HEAD_DIM]

    def project(t, positional):
        Bt, N, _ = t.shape
        parts = split_cols(t @ w_in, sizes)
        cq, ck, cv, dq, dk, dv = [u.reshape(Bt, N, -1, HEAD_DIM) for u in parts]
        if positional:
            cq = apply_rope(cq, *rope_h)
            ck = apply_rope(ck, *rope_h)
        return cq, ck, cv, dq, dk, dv

    B, S, _ = h.shape
    sink_g = sink.reshape(1, SWA_KV_HEADS, SWA_HEADS // SWA_KV_HEADS, 1, 1)
    cq, ck, cv, dq, dk, dv = project(h, True)
    cq_c, ck_c, cv_c, dq_c, dk_c, dv_c = project(hc, False)
    o_c = windowed_gqa(cq, ck, cv, ck_c, cv_c, sink_g)
    o_d = neighbourhood_attend(dq, dk, dv, dk_c, dv_c, rpb, nbr_idx, rel_idx)
    y = jnp.concatenate([o_c.reshape(B, S, -1), o_d.reshape(B, S, -1)], axis=-1) @ w_out
    yc = None
    if with_ctx:
        Bc, N = hc.shape[:2]
        oc_c = dense_gqa(cq_c, ck_c, cv_c, sink_g)
        oc_d = dense_gqa(dq_c, dk_c, dv_c)
        yc = jnp.concatenate([oc_c.reshape(Bc, N, -1), oc_d.reshape(Bc, N, -1)], axis=-1) @ w_out
    return y, yc


def setup_inputs(seed: int = 0) -> dict:
    key = jax.random.key(seed)
    ks = jax.random.split(key, 24)
    f32 = jnp.float32

    def nrm(k, shape, fan_in, s=1.0):
        return jax.random.normal(k, shape, f32) * (s * fan_in ** -0.5)

    def gain(k, shape):
        return 1.0 + 0.1 * jax.random.normal(k, shape, f32)

    return {
        'x': jax.random.normal(ks[0], (BATCH, SEQ, D_MODEL), f32),
        'c': jax.random.normal(ks[1], (BATCH, D_MODEL), f32),
        'ctx': jax.random.normal(ks[2], (BATCH, CTX_LEN, D_MODEL), f32),
        'c_ctx': jax.random.normal(ks[3], (D_MODEL,), f32),
        'ada_w': nrm(ks[4], (DEPTH, D_MODEL, N_MOD * D_MODEL), D_MODEL, 0.5),
        'ada_b': 0.02 * jax.random.normal(ks[5], (DEPTH, N_MOD * D_MODEL), f32),
        'norm_g': gain(ks[6], (DEPTH, 6, D_MODEL)),
        'ffn_w_gate': nrm(ks[7], (DEPTH, 2, D_MODEL, D_FF), D_MODEL),
        'ffn_w_up': nrm(ks[8], (DEPTH, 2, D_MODEL, D_FF), D_MODEL),
        'ffn_w_down': nrm(ks[9], (DEPTH, 2, D_FF, D_MODEL), D_FF),
        'ab_w_in': nrm(ks[10], (N_AB, D_MODEL, AB_IN), D_MODEL),
        'mla_q_norm': gain(ks[11], (N_AB, MLA_Q_RANK)),
        'mla_w_uq': nrm(ks[12], (N_AB, MLA_Q_RANK, MLA_HEADS * (MLA_NOPE + MLA_ROPE)), MLA_Q_RANK),
        'mla_kv_norm': gain(ks[13], (N_AB, MLA_KV_RANK)),
        'mla_w_ukv': nrm(ks[14], (N_AB, MLA_KV_RANK, MLA_HEADS * (MLA_NOPE + MLA_V)), MLA_KV_RANK),
        'gqa_q_norm': gain(ks[15], (N_AB, HEAD_DIM)),
        'gqa_k_norm': gain(ks[16], (N_AB, HEAD_DIM)),
        'ab_w_out': nrm(ks[17], (N_AB, AB_OUT, D_MODEL), AB_OUT),
        'cd_w_in': nrm(ks[18], (N_CD, D_MODEL, CD_IN), D_MODEL),
        'swa_sink': jax.random.normal(ks[19], (N_CD, SWA_HEADS), f32),
        'na_rpb': 0.1 * jax.random.normal(ks[20], (N_CD, NA_HEADS, 2 * NA_ROWS - 1, 2 * NA_COLS - 1), f32),
        'cd_w_out': nrm(ks[21], (N_CD, CD_OUT, D_MODEL), CD_OUT),
    }


def reference(x, c, ctx, c_ctx, ada_w, ada_b, norm_g, ffn_w_gate, ffn_w_up, ffn_w_down,
              ab_w_in, mla_q_norm, mla_w_uq, mla_kv_norm, mla_w_ukv, gqa_q_norm, gqa_k_norm,
              ab_w_out, cd_w_in, swa_sink, na_rpb, cd_w_out):
    B, S, _ = x.shape
    rows = S // GRID_W
    rope_a = axial_rope_tables(S, MLA_ROPE, x.dtype)
    rope_h = axial_rope_tables(S, HEAD_DIM, x.dtype)
    nbr_idx, rel_idx = neighbourhood_tables(rows)
    s_lat = jax.nn.silu(c)
    s_ctx = jax.nn.silu(c_ctx)[None]
    xc = ctx
    for l in range(DEPTH):
        with_ctx = l < DEPTH - 1
        mod = (s_lat @ ada_w[l] + ada_b[l]).reshape(B, N_MOD, -1)
        mod_c = (s_ctx @ ada_w[l] + ada_b[l]).reshape(1, N_MOD, -1)
        g = norm_g[l]
        w1 = (ffn_w_gate[l, 0], ffn_w_up[l, 0], ffn_w_down[l, 0])
        w2 = (ffn_w_gate[l, 1], ffn_w_up[l, 1], ffn_w_down[l, 1])
        x = ada_out(x, swiglu(ada_in(x, mod, 0, g[0]), *w1), mod, 0, g[1], FFN_RES)
        xc = ada_out(xc, swiglu(ada_in(xc, mod_c, 0, g[0]), *w1), mod_c, 0, g[1], FFN_RES)
        h = ada_in(x, mod, 1, g[2])
        hc = ada_in(xc, mod_c, 1, g[2])
        i = l // 2
        if l % 2 == 0:
            y, yc = ab_mixer(h, hc, ab_w_in[i], mla_q_norm[i], mla_w_uq[i], mla_kv_norm[i],
                             mla_w_ukv[i], gqa_q_norm[i], gqa_k_norm[i], ab_w_out[i],
                             rope_a, rope_h, with_ctx)
        else:
            y, yc = cd_mixer(h, hc, cd_w_in[i], swa_sink[i], na_rpb[i], cd_w_out[i],
                             rope_h, nbr_idx, rel_idx, with_ctx)
        x = ada_out(x, y, mod, 1, g[3], 1.0)
        x = ada_out(x, swiglu(ada_in(x, mod, 2, g[4]), *w2), mod, 2, g[5], FFN_RES)
        if with_ctx:
            xc = ada_out(xc, yc, mod_c, 1, g[3], 1.0)
            xc = ada_out(xc, swiglu(ada_in(xc, mod_c, 2, g[4]), *w2), mod_c, 2, g[5], FFN_RES)
    return x
```

```python
import functools

import jax
import jax.numpy as jnp
from jax import lax
from jax.experimental import pallas as pl
from jax.experimental.pallas import tpu as pltpu

GRID_W = 64
HEAD_DIM = 64
ROPE_BASE = 10000.0
EPS = 1e-6
NEG = -1e30
N_MOD = 9
FFN_RES = 0.5
MLA_HEADS = 8
MLA_Q_RANK = 256
MLA_KV_RANK = 128
MLA_NOPE = 64
MLA_ROPE = 32
MLA_V = 64
GQA_HEADS = 8
GQA_KV_HEADS = 2
SWA_HEADS = 8
SWA_KV_HEADS = 2
WINDOW = 128
NA_HEADS = 8
NA_ROWS = 8
NA_COLS = 16
LOG2E = 1.4426950408889634

LANE = 128
TM = 512
TQ_WIN = 256
N_HEADS = 16
VMEM_LIMIT = 56 * 1024 * 1024

F32 = jnp.float32
BF16 = jnp.bfloat16


def _dot(a, b):
    return jnp.dot(a, b, preferred_element_type=F32)


def _dot_nt(a, b):
    return lax.dot_general(a, b, (((1,), (1,)), ((), ())), preferred_element_type=F32)


def _dot_tn(a, b):
    return lax.dot_general(a, b, (((0,), (0,)), ((), ())), preferred_element_type=F32)


def _rms_rows(x):
    return x * lax.rsqrt(jnp.mean(x * x, axis=-1, keepdims=True) + EPS)


def _const_spec(shape):
    nd = len(shape)
    return pl.BlockSpec(shape, lambda *_: (0,) * nd, pipeline_mode=pl.Buffered(1))


def _params(n_grid):
    return pltpu.CompilerParams(dimension_semantics=("arbitrary",) * n_grid, vmem_limit_bytes=VMEM_LIMIT)


def _mod_kernel(c_ref, w_ref, b_ref, o_ref):
    c = c_ref[...]
    s = c * (1.0 / (1.0 + jnp.exp(-c)))
    o_ref[0] = _dot(s.astype(BF16), w_ref[0].astype(BF16)) + b_ref[0]


def _modulation(c, c_ctx, ada_w, ada_b):
    depth, d, nd = ada_w.shape
    tn = 1024
    cc = jnp.zeros((16, d), F32).at[0].set(c[0]).at[1].set(c_ctx)
    out = pl.pallas_call(
        _mod_kernel,
        grid=(depth, nd // tn),
        in_specs=[pl.BlockSpec((16, d), lambda l, j: (0, 0)),
                  pl.BlockSpec((1, d, tn), lambda l, j: (l, 0, j)),
                  pl.BlockSpec((1, 1, tn), lambda l, j: (l, 0, j))],
        out_specs=pl.BlockSpec((1, 16, tn), lambda l, j: (l, 0, j)),
        out_shape=jax.ShapeDtypeStruct((depth, 16, nd), F32),
        compiler_params=_params(2),
        name="adaln_mod",
    )(cc, ada_w, ada_b.reshape(depth, 1, nd))
    return out[:, :2].reshape(depth, 2, N_MOD, d)


def _ffn_chunks(d_ff):
    step = 512
    return [(c0, min(step, d_ff - c0)) for c0 in range(0, d_ff, step)]


def _ffn_kernel(*refs, slot, with_outproj):
    if with_outproj:
        ot_ref, wo_ref, x_ref, mod_ref, g_ref, wg_ref, wu_ref, wd_ref, o_ref = refs
    else:
        x_ref, mod_ref, g_ref, wg_ref, wu_ref, wd_ref, o_ref = refs
    x = x_ref[...]
    m = mod_ref[0]
    g = g_ref[...]
    if with_outproj:
        y = _dot_tn(ot_ref[...], wo_ref[...])
        x = x + m[5:6] * (_rms_rows(y) * g[3:4])
    shift, scale, gate = m[3 * slot:3 * slot + 1], m[3 * slot + 1:3 * slot + 2], m[3 * slot + 2:3 * slot + 3]
    g_in, g_out = g[2 * slot:2 * slot + 1], g[2 * slot + 1:2 * slot + 2]
    h = (_rms_rows(x) * (g_in * (1.0 + scale)) + shift).astype(BF16)
    acc = None
    for c0, cw in _ffn_chunks(wg_ref.shape[1]):
        a = _dot(h, wg_ref[:, c0:c0 + cw])
        u = _dot(h, wu_ref[:, c0:c0 + cw])
        act = (a * (1.0 / (1.0 + jnp.exp(-a))) * u).astype(BF16)
        part = _dot(act, wd_ref[c0:c0 + cw, :])
        acc = part if acc is None else acc + part
    o_ref[...] = x + (FFN_RES * gate) * (_rms_rows(acc) * g_out)


def _ffn(xa, mod_l, g, wg, wu, wd, *, slot, n_lat_tiles, ot=None, wo=None, out_rows=None):
    s_pad, d = xa.shape
    d_ff = wg.shape[1]
    out_rows = s_pad if out_rows is None else out_rows
    n_tiles = out_rows // TM
    mod_spec = pl.BlockSpec((1, N_MOD, d), lambda i: (jnp.where(i >= n_lat_tiles, 1, 0), 0, 0))
    in_specs = [pl.BlockSpec((TM, d), lambda i: (i, 0)), mod_spec, _const_spec((6, d)),
                _const_spec((d, d_ff)), _const_spec((d, d_ff)), _const_spec((d_ff, d))]
    args = [xa, mod_l, g, wg, wu, wd]
    if ot is not None:
        in_specs = [pl.BlockSpec((ot.shape[0], TM), lambda i: (0, i)), _const_spec(wo.shape)] + in_specs
        args = [ot, wo] + args
    return pl.pallas_call(
        functools.partial(_ffn_kernel, slot=slot, with_outproj=ot is not None),
        grid=(n_tiles,),
        in_specs=in_specs,
        out_specs=pl.BlockSpec((TM, d), lambda i: (i, 0)),
        out_shape=jax.ShapeDtypeStruct((out_rows, d), F32),
        compiler_params=_params(1),
        name="ffn_outproj" if ot is not None else "ffn",
    )(*args)


def _axial_tables(n_tok, rot_dim):
    t = jnp.arange(n_tok, dtype=jnp.int32)
    row = (t // GRID_W).astype(F32)
    col = (t % GRID_W).astype(F32)
    per_axis = rot_dim // 2
    inv = ROPE_BASE ** (-jnp.arange(0, per_axis, 2, dtype=F32) / per_axis)
    ang = jnp.concatenate([row[:, None] * inv, col[:, None] * inv], axis=-1)
    return jnp.cos(ang), jnp.sin(ang)


def _stream_rope(s_lat, s_pad, rot_dim):
    cos, sin = _axial_tables(s_lat, rot_dim)
    n_extra = s_pad - s_lat
    cos_full = jnp.concatenate([jnp.concatenate([cos, cos], axis=-1), jnp.ones((n_extra, rot_dim), F32)], axis=0)
    sin_sgn = jnp.concatenate([jnp.concatenate([-sin, sin], axis=-1), jnp.zeros((n_extra, rot_dim), F32)], axis=0)
    return cos_full, sin_sgn


def _swap_halves(w, n_heads, dim):
    return jnp.roll(w.reshape(w.shape[0], n_heads, dim), dim // 2, axis=-1).reshape(w.shape[0], n_heads * dim)


def _ab_proj_kernel(x_ref, mod_ref, g_ref, w1_ref, w2t_ref, qn_ref, kvn_ref, wqt_ref, wuk_ref, wuvt_ref,
                    tt_ref, tn_ref, qt_ref, kn_ref, vt_ref):
    x = x_ref[...]
    m = mod_ref[0]
    g = g_ref[...]
    tm = x.shape[0]
    hb = (_rms_rows(x) * (g[2:3] * (1.0 + m[4:5])) + m[3:4]).astype(BF16)
    y1 = _dot(hb, w1_ref[...])
    y2 = _dot_nt(w2t_ref[...], hb)
    tt = tt_ref[...]
    tn = tn_ref[...]
    cqn = (_rms_rows(y1[:, 0:256]) * qn_ref[...]).astype(BF16)
    ckvn = (_rms_rows(y1[:, 256:384]) * kvn_ref[...]).astype(BF16)

    qa = _dot_nt(wqt_ref[...], cqn)
    qa_t = (qa[0:1024].reshape(MLA_HEADS, LANE, tm) * tt[0:128][None]
            + qa[1024:2048].reshape(MLA_HEADS, LANE, tm) * tt[128:256][None])
    qt_ref[0:1024, :] = (qa_t * ((MLA_NOPE + MLA_ROPE) ** -0.5 * LOG2E)).reshape(1024, tm).astype(BF16)
    k_nope = _dot(ckvn, wuk_ref[...])
    kr = y1[:, 384:512] * tn[:, 0:128] + y1[:, 512:640] * tn[:, 128:256]
    for h in range(MLA_HEADS):
        kn_ref[:, LANE * h:LANE * (h + 1)] = (k_nope[:, LANE * h:LANE * (h + 1)] + kr).astype(BF16)
    vt_ref[0, 0:512, :] = _dot_nt(wuvt_ref[...], ckvn).astype(BF16)

    aq = y2[0:512].reshape(GQA_HEADS, HEAD_DIM, tm)
    bq = y2[512:1024].reshape(GQA_HEADS, HEAD_DIM, tm)
    rq = lax.rsqrt(jnp.mean(aq * aq, axis=1, keepdims=True) + EPS)
    qb = ((aq * tt[256:320][None] + bq * tt[320:384][None]) * (rq * (HEAD_DIM ** -0.5 * LOG2E))).astype(BF16)
    zeros = jnp.zeros((HEAD_DIM, tm), BF16)
    group = GQA_HEADS // GQA_KV_HEADS
    for h in range(GQA_HEADS):
        base = 1024 + LANE * h
        for kv in range(GQA_KV_HEADS):
            qt_ref[base + HEAD_DIM * kv:base + HEAD_DIM * (kv + 1), :] = qb[h] if kv == h // group else zeros
    bk = y1[:, 640:768]
    lane = lax.broadcasted_iota(jnp.int32, bk.shape, 1)
    sq = bk * bk
    ms0 = jnp.sum(jnp.where(lane < HEAD_DIM, sq, 0.0), axis=-1, keepdims=True) * (1.0 / HEAD_DIM)
    ms1 = jnp.sum(jnp.where(lane >= HEAD_DIM, sq, 0.0), axis=-1, keepdims=True) * (1.0 / HEAD_DIM)
    rk = jnp.where(lane < HEAD_DIM, lax.rsqrt(ms0 + EPS), lax.rsqrt(ms1 + EPS))
    kn_ref[:, 1024:1152] = ((bk * tn[:, 256:384] + y1[:, 768:896] * tn[:, 384:512]) * rk).astype(BF16)
    vt_ref[0, 512:640, :] = y2[1024:1152].astype(BF16)


def _ab_proj(xa, mod_l, g, w, tt, tn, *, n_lat_tiles):
    s_pad, d = xa.shape
    n_tiles = s_pad // TM
    mod_spec = pl.BlockSpec((1, N_MOD, d), lambda i: (jnp.where(i >= n_lat_tiles, 1, 0), 0, 0))
    consts = [w["w1"], w["w2t"], w["qn"], w["kvn"], w["wqt"], w["wuk"], w["wuvt"]]
    return pl.pallas_call(
        _ab_proj_kernel,
        grid=(n_tiles,),
        in_specs=[pl.BlockSpec((TM, d), lambda i: (i, 0)), mod_spec, _const_spec((6, d))]
        + [_const_spec(a.shape) for a in consts]
        + [pl.BlockSpec((tt.shape[0], TM), lambda i: (0, i)), pl.BlockSpec((TM, tn.shape[1]), lambda i: (i, 0))],
        out_specs=[pl.BlockSpec((N_HEADS * LANE, TM), lambda i: (0, i)),
                   pl.BlockSpec((TM, 9 * LANE), lambda i: (i, 0)),
                   pl.BlockSpec((1, 10 * HEAD_DIM, TM), lambda i: (i, 0, 0))],
        out_shape=[jax.ShapeDtypeStruct((N_HEADS * LANE, s_pad), BF16),
                   jax.ShapeDtypeStruct((s_pad, 9 * LANE), BF16),
                   jax.ShapeDtypeStruct((n_tiles, 10 * HEAD_DIM, TM), BF16)],
        compiler_params=_params(1),
        name="ab_proj",
    )(xa, mod_l, g, *consts, tt, tn)


def _ab_weights(w_in, q_norm, w_uq, kv_norm, w_ukv):
    d = w_in.shape[0]
    o = 0
    parts = []
    for width in (MLA_Q_RANK, MLA_KV_RANK, MLA_ROPE, GQA_HEADS * HEAD_DIM, GQA_KV_HEADS * HEAD_DIM,
                  GQA_KV_HEADS * HEAD_DIM):
        parts.append(w_in[:, o:o + width])
        o += width
    w_cq, w_ckv, w_kr, w_bq, w_bk, w_bv = parts

    def rope_lanes(w):
        return jnp.zeros((w.shape[0], LANE), F32).at[:, MLA_NOPE:MLA_NOPE + MLA_ROPE].set(w)

    w1 = jnp.concatenate([w_cq, w_ckv, rope_lanes(w_kr), rope_lanes(_swap_halves(w_kr, 1, MLA_ROPE)),
                          w_bk, _swap_halves(w_bk, GQA_KV_HEADS, HEAD_DIM)], axis=1)
    w2t = jnp.concatenate([w_bq, _swap_halves(w_bq, GQA_HEADS, HEAD_DIM), w_bv], axis=1).T
    uq = w_uq.reshape(MLA_Q_RANK, MLA_HEADS, MLA_NOPE + MLA_ROPE)
    wa = jnp.zeros((MLA_Q_RANK, MLA_HEADS, LANE), F32).at[:, :, :MLA_NOPE + MLA_ROPE].set(uq)
    wr = jnp.zeros((MLA_Q_RANK, MLA_HEADS, LANE), F32).at[:, :, MLA_NOPE:MLA_NOPE + MLA_ROPE].set(
        jnp.roll(uq[:, :, MLA_NOPE:], MLA_ROPE // 2, axis=-1))
    wqt = jnp.concatenate([wa.reshape(MLA_Q_RANK, -1), wr.reshape(MLA_Q_RANK, -1)], axis=1).T
    ukv = w_ukv.reshape(MLA_KV_RANK, MLA_HEADS, MLA_NOPE + MLA_V)
    wuk = jnp.zeros((MLA_KV_RANK, MLA_HEADS, LANE), F32).at[:, :, :MLA_NOPE].set(ukv[:, :, :MLA_NOPE])
    wuvt = ukv[:, :, MLA_NOPE:].reshape(MLA_KV_RANK, MLA_HEADS * MLA_V).T
    return dict(w1=w1.astype(BF16), w2t=w2t.astype(BF16), qn=q_norm[None, :], kvn=kv_norm[None, :],
                wqt=wqt.astype(BF16), wuk=wuk.reshape(MLA_KV_RANK, -1).astype(BF16), wuvt=wuvt.astype(BF16))


def _ab_tables(rope_a, rope_h, qn_b, kn_b):
    cos_a, sin_a = rope_a
    cos_h, sin_h = rope_h
    n = cos_a.shape[0]
    cos_a_full = jnp.concatenate([jnp.ones((n, MLA_NOPE), F32), cos_a, jnp.ones((n, LANE - 96), F32)], axis=1)
    sin_a_full = jnp.concatenate([jnp.zeros((n, MLA_NOPE), F32), sin_a, jnp.zeros((n, LANE - 96), F32)], axis=1)
    half = HEAD_DIM // 2
    cgq, sgq = cos_h * qn_b[None, :], sin_h * jnp.roll(qn_b, half)[None, :]
    cgk, sgk = cos_h * kn_b[None, :], sin_h * jnp.roll(kn_b, half)[None, :]
    tt = jnp.concatenate([cos_a_full, sin_a_full, cgq, sgq], axis=1).T
    tn = jnp.concatenate([cos_a_full, sin_a_full, cgk, cgk, sgk, sgk], axis=1)
    return tt, tn


def _flash_kernel(*refs, n_lat_tiles, n_chunks, s_lat, ctx_len, q_off, use_sink):
    if use_sink:
        sink_ref, qt_ref, k_ref, vt_ref, _, o_ref = refs
    else:
        qt_ref, k_ref, vt_ref, o_ref = refs
    tile = pl.program_id(1) + q_off
    qt = qt_ref[...]
    tq = qt.shape[1]

    def attend(k_blk, vt_blk, carry):
        m, l, acc = carry
        s = _dot(k_blk, qt)
        m_new = jnp.maximum(m, jnp.max(s, axis=0, keepdims=True))
        alpha = jnp.exp2(m - m_new)
        p = jnp.exp2(s - m_new)
        l = alpha * l + jnp.sum(p, axis=0, keepdims=True)
        acc = alpha * acc + _dot(vt_blk, p.astype(BF16))
        return m_new, l, acc

    def body(c, carry):
        r0 = pl.multiple_of(c * TM, TM)
        return attend(k_ref[pl.ds(r0, TM), :], vt_ref[c], carry)

    init = (jnp.full((1, tq), NEG, F32), jnp.zeros((1, tq), F32), jnp.zeros((HEAD_DIM, tq), F32))
    carry = lax.fori_loop(0, jnp.where(tile < n_lat_tiles, n_chunks, 0), body, init)
    m, l, acc = attend(k_ref[s_lat:s_lat + ctx_len, :], vt_ref[s_lat // TM][:, 0:ctx_len], carry)
    if use_sink:
        sink = sink_ref[pl.program_id(0)]
        m_fin = jnp.maximum(m, sink)
        alpha = jnp.exp2(m - m_fin)
        l = alpha * l + jnp.exp2(sink - m_fin)
        acc = alpha * acc
    o_ref[...] = (acc / l).astype(BF16)


def _flash(qt, kn, vt, kb, vb, *, s_lat, ctx_len, q_tiles, q_off=0, sink=None, o_prev=None):
    s_pad = qt.shape[1]
    n_lat_tiles = s_lat // TM
    kern = functools.partial(_flash_kernel, n_lat_tiles=n_lat_tiles, n_chunks=s_lat // TM, s_lat=s_lat,
                             ctx_len=ctx_len, q_off=q_off, use_sink=sink is not None)
    in_specs = [pl.BlockSpec((LANE, TM), lambda h, i: (h, i + q_off)),
                pl.BlockSpec((s_pad, LANE), lambda h, i: (0, kb(h))),
                pl.BlockSpec((s_pad // TM, HEAD_DIM, TM), lambda h, i: (0, vb(h), 0))]
    args = [qt, kn, vt]
    aliases = {}
    if sink is not None:
        in_specs = [pl.BlockSpec(memory_space=pltpu.SMEM)] + in_specs + [pl.BlockSpec(memory_space=pl.ANY)]
        args = [sink] + args + [o_prev]
        aliases = {4: 0}
    return pl.pallas_call(
        kern,
        grid=(N_HEADS, q_tiles),
        in_specs=in_specs,
        out_specs=pl.BlockSpec((HEAD_DIM, TM), lambda h, i: (h, i + q_off)),
        out_shape=jax.ShapeDtypeStruct((N_HEADS * HEAD_DIM, s_pad), BF16),
        input_output_aliases=aliases,
        compiler_params=_params(2),
        name="flash_ctx" if sink is not None else "flash",
    )(*args)


def _cd_proj_kernel(x_ref, mod_ref, g_ref, w1_ref, w2t_ref, tt_ref, tn_ref, qt_ref, kn_ref, vt_ref):
    x = x_ref[...]
    m = mod_ref[0]
    g = g_ref[...]
    tm = x.shape[0]
    hb = (_rms_rows(x) * (g[2:3] * (1.0 + m[4:5])) + m[3:4]).astype(BF16)
    y1 = _dot(hb, w1_ref[...])
    y2 = _dot_nt(w2t_ref[...], hb)
    tt = tt_ref[...]
    tn = tn_ref[...]
    scale = HEAD_DIM ** -0.5 * LOG2E
    qc = (y2[0:1024].reshape(SWA_HEADS, LANE, tm) * tt[0:128][None]
          + y2[2048:3072].reshape(SWA_HEADS, LANE, tm) * tt[128:256][None])
    qt_ref[0:1024, :] = (qc * scale).reshape(1024, tm).astype(BF16)
    qt_ref[1024:2048, :] = (y2[1024:2048] * scale).astype(BF16)
    kn_ref[:, 0:128] = (y1[:, 0:128] * tn[:, 0:128] + y1[:, 128:256] * tn[:, 128:256]).astype(BF16)
    kn_ref[:, 128:640] = y1[:, 256:768].astype(BF16)
    vt_ref[0] = y2[3072:3712].astype(BF16)


def _cd_proj(xa, mod_l, g, w, tt, tn, *, n_lat_tiles):
    s_pad, d = xa.shape
    n_tiles = s_pad // TM
    mod_spec = pl.BlockSpec((1, N_MOD, d), lambda i: (jnp.where(i >= n_lat_tiles, 1, 0), 0, 0))
    return pl.pallas_call(
        _cd_proj_kernel,
        grid=(n_tiles,),
        in_specs=[pl.BlockSpec((TM, d), lambda i: (i, 0)), mod_spec, _const_spec((6, d)),
                  _const_spec(w["w1"].shape), _const_spec(w["w2t"].shape),
                  pl.BlockSpec((tt.shape[0], TM), lambda i: (0, i)), pl.BlockSpec((TM, tn.shape[1]), lambda i: (i, 0))],
        out_specs=[pl.BlockSpec((N_HEADS * LANE, TM), lambda i: (0, i)),
                   pl.BlockSpec((TM, 5 * LANE), lambda i: (i, 0)),
                   pl.BlockSpec((1, 10 * HEAD_DIM, TM), lambda i: (i, 0, 0))],
        out_shape=[jax.ShapeDtypeStruct((N_HEADS * LANE, s_pad), BF16),
                   jax.ShapeDtypeStruct((s_pad, 5 * LANE), BF16),
                   jax.ShapeDtypeStruct((n_tiles, 10 * HEAD_DIM, TM), BF16)],
        compiler_params=_params(1),
        name="cd_proj",
    )(xa, mod_l, g, w["w1"], w["w2t"], tt, tn)


def _cd_weights(w_in):
    d = w_in.shape[0]
    o = 0
    parts = []
    for width in (SWA_HEADS * HEAD_DIM, SWA_KV_HEADS * HEAD_DIM, SWA_KV_HEADS * HEAD_DIM,
                  NA_HEADS * HEAD_DIM, NA_HEADS * HEAD_DIM, NA_HEADS * HEAD_DIM):
        parts.append(w_in[:, o:o + width])
        o += width
    w_cq, w_ck, w_cv, w_dq, w_dk, w_dv = parts

    def pad_heads(w, n_heads, slot_of_head):
        w3 = w.reshape(d, n_heads, HEAD_DIM)
        out = jnp.zeros((d, n_heads, 2, HEAD_DIM), F32)
        for h in range(n_heads):
            out = out.at[:, h, slot_of_head(h)].set(w3[:, h])
        return out.reshape(d, n_heads * LANE)

    group = SWA_HEADS // SWA_KV_HEADS
    q_c = pad_heads(w_cq, SWA_HEADS, lambda h: h // group)
    q_c_rot = pad_heads(_swap_halves(w_cq, SWA_HEADS, HEAD_DIM), SWA_HEADS, lambda h: h // group)
    q_d = pad_heads(w_dq, NA_HEADS, lambda h: h % 2)
    w1 = jnp.concatenate([w_ck, _swap_halves(w_ck, SWA_KV_HEADS, HEAD_DIM), w_dk], axis=1)
    w2t = jnp.concatenate([q_c, q_d, q_c_rot, w_cv, w_dv], axis=1).T
    return dict(w1=w1.astype(BF16), w2t=w2t.astype(BF16))


def _cd_tables(rope_h):
    cos_h, sin_h = rope_h
    tn = jnp.concatenate([cos_h, cos_h, sin_h, sin_h], axis=1)
    return tn.T, tn


def _na_bias_tables(rpb, rows, n_tiles):
    rows_per_tile = TQ_WIN // GRID_W
    kidx = jnp.arange(3 * TQ_WIN, dtype=jnp.int32)[:, None]
    qidx = jnp.arange(TQ_WIN, dtype=jnp.int32)[None, :]
    out = []
    for tile in (0, 1, n_tiles - 1):
        start = min(max(tile - 1, 0), n_tiles - 3)
        kr, kc = start * rows_per_tile + kidx // GRID_W, kidx % GRID_W
        r, col = tile * rows_per_tile + qidx // GRID_W, qidx % GRID_W
        r0 = jnp.clip(r - NA_ROWS // 2, 0, rows - NA_ROWS)
        c0 = jnp.clip(col - NA_COLS // 2, 0, GRID_W - NA_COLS)
        valid = (kr >= r0) & (kr < r0 + NA_ROWS) & (kc >= c0) & (kc < c0 + NA_COLS)
        rel_r = jnp.clip(kr - r + NA_ROWS - 1, 0, 2 * NA_ROWS - 2)
        rel_c = jnp.clip(kc - col + NA_COLS - 1, 0, 2 * NA_COLS - 2)
        out.append(jnp.where(valid[None], rpb[:, rel_r, rel_c] * LOG2E, NEG))
    return jnp.stack(out)


def _cd_attn_kernel(sink_ref, qt_ref, k0_ref, k1_ref, k2_ref, kc_ref, v0_ref, v1_ref, v2_ref, vc_ref, tab_ref,
                    o_ref, *, n_tiles):
    h = pl.program_id(0)
    i = pl.program_id(1)
    start = jnp.clip(i - 1, 0, n_tiles - 3)
    qt = qt_ref[...]
    kw = jnp.concatenate([k0_ref[...], k1_ref[...], k2_ref[...]], axis=0)
    sw = _dot(kw, qt)
    sc = _dot(kc_ref[...], qt)
    kpos = start * TQ_WIN + lax.broadcasted_iota(jnp.int32, sw.shape, 0)
    qpos = i * TQ_WIN + lax.broadcasted_iota(jnp.int32, sw.shape, 1)
    band = jnp.where(jnp.abs(qpos - kpos) <= WINDOW, 0.0, NEG)
    sw = sw + jnp.where(h < SWA_HEADS, band, tab_ref[0, 0])
    sink = sink_ref[h]
    m = jnp.maximum(jnp.maximum(jnp.max(sw, axis=0, keepdims=True), jnp.max(sc, axis=0, keepdims=True)), sink)
    pw = jnp.exp2(sw - m)
    pc = jnp.exp2(sc - m)
    l = jnp.sum(pw, axis=0, keepdims=True) + jnp.sum(pc, axis=0, keepdims=True) + jnp.exp2(sink - m)
    vw = jnp.concatenate([v0_ref[0], v1_ref[0], v2_ref[0]], axis=1)
    acc = _dot(vw, pw.astype(BF16)) + _dot(vc_ref[0], pc.astype(BF16))
    o_ref[...] = (acc / l).astype(BF16)


def _cd_attn(qt, kn, vt, tab, sink, kb, vb, *, s_lat, ctx_len):
    s_pad = qt.shape[1]
    n_tiles = s_lat // TQ_WIN
    per_chunk = TM // TQ_WIN

    def start(i):
        return jnp.clip(i - 1, 0, n_tiles - 3)

    def k_spec(j):
        return pl.BlockSpec((TQ_WIN, LANE), lambda h, i: (start(i) + j, kb(h)))

    def v_spec(j):
        return pl.BlockSpec((1, HEAD_DIM, TQ_WIN),
                            lambda h, i: ((start(i) + j) // per_chunk, vb(h), (start(i) + j) % per_chunk))

    def tab_map(h, i):
        return (jnp.where(i == 0, 0, jnp.where(i == n_tiles - 1, 2, 1)), jnp.maximum(h - SWA_HEADS, 0), 0, 0)

    return pl.pallas_call(
        functools.partial(_cd_attn_kernel, n_tiles=n_tiles),
        grid=(N_HEADS, n_tiles),
        in_specs=[pl.BlockSpec(memory_space=pltpu.SMEM),
                  pl.BlockSpec((LANE, TQ_WIN), lambda h, i: (h, i)),
                  k_spec(0), k_spec(1), k_spec(2),
                  pl.BlockSpec((ctx_len, LANE), lambda h, i: (s_lat // ctx_len, kb(h))),
                  v_spec(0), v_spec(1), v_spec(2),
                  pl.BlockSpec((1, HEAD_DIM, ctx_len), lambda h, i: (s_lat // TM, vb(h), 0)),
                  pl.BlockSpec((1, 1, 3 * TQ_WIN, TQ_WIN), tab_map)],
        out_specs=pl.BlockSpec((HEAD_DIM, TQ_WIN), lambda h, i: (h, i)),
        out_shape=jax.ShapeDtypeStruct((N_HEADS * HEAD_DIM, s_pad), BF16),
        compiler_params=_params(2),
        name="cd_attn",
    )(sink, qt, kn, kn, kn, kn, vt, vt, vt, vt, tab)


def _ab_kb(h):
    return jnp.minimum(h, MLA_HEADS)


def _ab_vb(h):
    return jnp.where(h < MLA_HEADS, h, MLA_HEADS + (h - MLA_HEADS) // (GQA_HEADS // GQA_KV_HEADS))


def _cd_kb(h):
    return jnp.where(h < SWA_HEADS, 0, 1 + (h - SWA_HEADS) // 2)


def _cd_vb(h):
    return jnp.where(h < SWA_HEADS, h // (SWA_HEADS // SWA_KV_HEADS), SWA_KV_HEADS + h - SWA_HEADS)


def kernel(x, c, ctx, c_ctx, ada_w, ada_b, norm_g, ffn_w_gate, ffn_w_up, ffn_w_down, ab_w_in, mla_q_norm, mla_w_uq,
           mla_kv_norm, mla_w_ukv, gqa_q_norm, gqa_k_norm, ab_w_out, cd_w_in, swa_sink, na_rpb, cd_w_out):
    batch, s_lat, d = x.shape
    ctx_len = ctx.shape[1]
    depth = ada_w.shape[0]
    rows = s_lat // GRID_W
    assert batch == 1 and c.shape[0] == 1 and ctx.shape[0] == 1
    assert s_lat % TM == 0 and s_lat % ctx_len == 0 and ctx_len % LANE == 0 and ctx_len <= TM
    assert s_lat // TQ_WIN >= 3 and rows >= NA_ROWS + TQ_WIN // GRID_W
    s_pad = s_lat + TM
    n_lat_tiles = s_lat // TM

    xa = jnp.concatenate([x[0], ctx[0], jnp.zeros((TM - ctx_len, d), F32)], axis=0)
    mods = _modulation(c, c_ctx, ada_w, ada_b)
    rope_a = _stream_rope(s_lat, s_pad, MLA_ROPE)
    rope_h = _stream_rope(s_lat, s_pad, HEAD_DIM)
    cd_tt, cd_tn = _cd_tables(rope_h)

    for l in range(depth):
        last = l == depth - 1
        mod_l, g = mods[l], norm_g[l]
        wg, wu, wd = ffn_w_gate[l].astype(BF16), ffn_w_up[l].astype(BF16), ffn_w_down[l].astype(BF16)
        xa = _ffn(xa, mod_l, g, wg[0], wu[0], wd[0], slot=0, n_lat_tiles=n_lat_tiles)
        j = l // 2
        if l % 2 == 0:
            w = _ab_weights(ab_w_in[j], mla_q_norm[j], mla_w_uq[j], mla_kv_norm[j], mla_w_ukv[j])
            tt, tn = _ab_tables(rope_a, rope_h, gqa_q_norm[j], gqa_k_norm[j])
            qt, kn, vt = _ab_proj(xa, mod_l, g, w, tt, tn, n_lat_tiles=n_lat_tiles)
            ot = _flash(qt, kn, vt, _ab_kb, _ab_vb, s_lat=s_lat, ctx_len=ctx_len,
                        q_tiles=n_lat_tiles + (0 if last else 1))
            wo = ab_w_out[j].astype(BF16)
        else:
            w = _cd_weights(cd_w_in[j])
            qt, kn, vt = _cd_proj(xa, mod_l, g, w, cd_tt, cd_tn, n_lat_tiles=n_lat_tiles)
            tab = _na_bias_tables(na_rpb[j], rows, s_lat // TQ_WIN)
            sink = jnp.concatenate([swa_sink[j] * LOG2E, jnp.full((NA_HEADS,), NEG, F32)])
            ot = _cd_attn(qt, kn, vt, tab, sink, _cd_kb, _cd_vb, s_lat=s_lat, ctx_len=ctx_len)
            if not last:
                ot = _flash(qt, kn, vt, _cd_kb, _cd_vb, s_lat=s_lat, ctx_len=ctx_len, q_tiles=1,
                            q_off=n_lat_tiles, sink=sink, o_prev=ot)
            wo = cd_w_out[j].astype(BF16)
        xa = _ffn(xa, mod_l, g, wg[1], wu[1], wd[1], slot=2, n_lat_tiles=n_lat_tiles, ot=ot, wo=wo,
                  out_rows=s_lat if last else None)
    return xa[None]
```

```python
import functools

import jax
import jax.numpy as jnp
from jax import lax
from jax.experimental import pallas as pl
from jax.experimental.pallas import tpu as pltpu

GRID_W = 64
HEAD_DIM = 64
ROPE_BASE = 10000.0
EPS = 1e-6
NEG = -1e30
N_MOD = 9
FFN_RES = 0.5
MLA_HEADS = 8
MLA_Q_RANK = 256
MLA_KV_RANK = 128
MLA_NOPE = 64
MLA_ROPE = 32
MLA_V = 64
GQA_HEADS = 8
GQA_KV_HEADS = 2
SWA_HEADS = 8
SWA_KV_HEADS = 2
WINDOW = 128
NA_HEADS = 8
NA_ROWS = 8
NA_COLS = 16
LOG2E = 1.4426950408889634

LANE = 128
TM = 512
TQ_WIN = 256
N_HEADS = 16
VMEM_LIMIT = 56 * 1024 * 1024

F32 = jnp.float32
BF16 = jnp.bfloat16


def _dot(a, b):
    return jnp.dot(a, b, preferred_element_type=F32)


def _dot_nt(a, b):
    return lax.dot_general(a, b, (((1,), (1,)), ((), ())), preferred_element_type=F32)


def _dot_tn(a, b):
    return lax.dot_general(a, b, (((0,), (0,)), ((), ())), preferred_element_type=F32)


def _rms_rows(x):
    return x * lax.rsqrt(jnp.mean(x * x, axis=-1, keepdims=True) + EPS)


def _const_spec(shape):
    nd = len(shape)
    return pl.BlockSpec(shape, lambda *_: (0,) * nd, pipeline_mode=pl.Buffered(1))


def _params(n_grid):
    return pltpu.CompilerParams(dimension_semantics=("arbitrary",) * n_grid, vmem_limit_bytes=VMEM_LIMIT)


def _mod_kernel(c_ref, w_ref, b_ref, o_ref):
    c = c_ref[...]
    s = c * (1.0 / (1.0 + jnp.exp(-c)))
    o_ref[0] = _dot(s.astype(BF16), w_ref[0].astype(BF16)) + b_ref[0]


def _modulation(c, c_ctx, ada_w, ada_b):
    depth, d, nd = ada_w.shape
    tn = 1024
    cc = jnp.zeros((16, d), F32).at[0].set(c[0]).at[1].set(c_ctx)
    out = pl.pallas_call(
        _mod_kernel,
        grid=(depth, nd // tn),
        in_specs=[pl.BlockSpec((16, d), lambda l, j: (0, 0)),
                  pl.BlockSpec((1, d, tn), lambda l, j: (l, 0, j)),
                  pl.BlockSpec((1, 1, tn), lambda l, j: (l, 0, j))],
        out_specs=pl.BlockSpec((1, 16, tn), lambda l, j: (l, 0, j)),
        out_shape=jax.ShapeDtypeStruct((depth, 16, nd), F32),
        compiler_params=_params(2),
        name="adaln_mod",
    )(cc, ada_w, ada_b.reshape(depth, 1, nd))
    return out[:, :2].reshape(depth, 2, N_MOD, d)


def _ffn_chunks(d_ff):
    step = 512
    return [(c0, min(step, d_ff - c0)) for c0 in range(0, d_ff, step)]


def _ffn_kernel(*refs, slot, with_outproj):
    if with_outproj:
        ot_ref, wo_ref, x_ref, mod_ref, g_ref, wg_ref, wu_ref, wd_ref, o_ref = refs
    else:
        x_ref, mod_ref, g_ref, wg_ref, wu_ref, wd_ref, o_ref = refs
    x = x_ref[...]
    m = mod_ref[0]
    g = g_ref[...]
    if with_outproj:
        y = _dot_tn(ot_ref[...], wo_ref[...])
        x = x + m[5:6] * (_rms_rows(y) * g[3:4])
    shift, scale, gate = m[3 * slot:3 * slot + 1], m[3 * slot + 1:3 * slot + 2], m[3 * slot + 2:3 * slot + 3]
    g_in, g_out = g[2 * slot:2 * slot + 1], g[2 * slot + 1:2 * slot + 2]
    h = (_rms_rows(x) * (g_in * (1.0 + scale)) + shift).astype(BF16)
    acc = None
    for c0, cw in _ffn_chunks(wg_ref.shape[1]):
        a = _dot(h, wg_ref[:, c0:c0 + cw])
        u = _dot(h, wu_ref[:, c0:c0 + cw])
        act = (a * (1.0 / (1.0 + jnp.exp(-a))) * u).astype(BF16)
        part = _dot(act, wd_ref[c0:c0 + cw, :])
        acc = part if acc is None else acc + part
    o_ref[...] = x + (FFN_RES * gate) * (_rms_rows(acc) * g_out)


def _ffn(xa, mod_l, g, wg, wu, wd, *, slot, n_lat_tiles, ot=None, wo=None, out_rows=None):
    s_pad, d = xa.shape
    d_ff = wg.shape[1]
    out_rows = s_pad if out_rows is None else out_rows
    n_tiles = out_rows // TM
    mod_spec = pl.BlockSpec((1, N_MOD, d), lambda i: (jnp.where(i >= n_lat_tiles, 1, 0), 0, 0))
    in_specs = [pl.BlockSpec((TM, d), lambda i: (i, 0)), mod_spec, _const_spec((6, d)),
                _const_spec((d, d_ff)), _const_spec((d, d_ff)), _const_spec((d_ff, d))]
    args = [xa, mod_l, g, wg, wu, wd]
    if ot is not None:
        in_specs = [pl.BlockSpec((ot.shape[0], TM), lambda i: (0, i)), _const_spec(wo.shape)] + in_specs
        args = [ot, wo] + args
    return pl.pallas_call(
        functools.partial(_ffn_kernel, slot=slot, with_outproj=ot is not None),
        grid=(n_tiles,),
        in_specs=in_specs,
        out_specs=pl.BlockSpec((TM, d), lambda i: (i, 0)),
        out_shape=jax.ShapeDtypeStruct((out_rows, d), F32),
        compiler_params=_params(1),
        name="ffn_outproj" if ot is not None else "ffn",
    )(*args)


def _axial_tables(n_tok, rot_dim):
    t = jnp.arange(n_tok, dtype=jnp.int32)
    row = (t // GRID_W).astype(F32)
    col = (t % GRID_W).astype(F32)
    per_axis = rot_dim // 2
    inv = ROPE_BASE ** (-jnp.arange(0, per_axis, 2, dtype=F32) / per_axis)
    ang = jnp.concatenate([row[:, None] * inv, col[:, None] * inv], axis=-1)
    return jnp.cos(ang), jnp.sin(ang)


def _stream_rope(s_lat, s_pad, rot_dim):
    cos, sin = _axial_tables(s_lat, rot_dim)
    n_extra = s_pad - s_lat
    cos_full = jnp.concatenate([jnp.concatenate([cos, cos], axis=-1), jnp.ones((n_extra, rot_dim), F32)], axis=0)
    sin_sgn = jnp.concatenate([jnp.concatenate([-sin, sin], axis=-1), jnp.zeros((n_extra, rot_dim), F32)], axis=0)
    return cos_full, sin_sgn


def _swap_halves(w, n_heads, dim):
    return jnp.roll(w.reshape(w.shape[0], n_heads, dim), dim // 2, axis=-1).reshape(w.shape[0], n_heads * dim)


def _ab_proj_kernel(x_ref, mod_ref, g_ref, w1_ref, w2t_ref, qn_ref, kvn_ref, wqt_ref, wuk_ref, wuvt_ref,
                    tt_ref, tn_ref, qt_ref, kn_ref, vt_ref):
    x = x_ref[...]
    m = mod_ref[0]
    g = g_ref[...]
    tm = x.shape[0]
    hb = (_rms_rows(x) * (g[2:3] * (1.0 + m[4:5])) + m[3:4]).astype(BF16)
    y1 = _dot(hb, w1_ref[...])
    y2 = _dot_nt(w2t_ref[...], hb)
    tt = tt_ref[...]
    tn = tn_ref[...]
    cqn = (_rms_rows(y1[:, 0:256]) * qn_ref[...]).astype(BF16)
    ckvn = (_rms_rows(y1[:, 256:384]) * kvn_ref[...]).astype(BF16)

    qa = _dot_nt(wqt_ref[...], cqn)
    qa_t = (qa[0:1024].reshape(MLA_HEADS, LANE, tm) * tt[0:128][None]
            + qa[1024:2048].reshape(MLA_HEADS, LANE, tm) * tt[128:256][None])
    qt_ref[0:1024, :] = (qa_t * ((MLA_NOPE + MLA_ROPE) ** -0.5 * LOG2E)).reshape(1024, tm).astype(BF16)
    k_nope = _dot(ckvn, wuk_ref[...])
    kr = y1[:, 384:512] * tn[:, 0:128] + y1[:, 512:640] * tn[:, 128:256]
    for h in range(MLA_HEADS):
        kn_ref[:, LANE * h:LANE * (h + 1)] = (k_nope[:, LANE * h:LANE * (h + 1)] + kr).astype(BF16)
    vt_ref[0, 0:512, :] = _dot_nt(wuvt_ref[...], ckvn).astype(BF16)

    aq = y2[0:512].reshape(GQA_HEADS, HEAD_DIM, tm)
    bq = y2[512:1024].reshape(GQA_HEADS, HEAD_DIM, tm)
    rq = lax.rsqrt(jnp.mean(aq * aq, axis=1, keepdims=True) + EPS)
    qb = ((aq * tt[256:320][None] + bq * tt[320:384][None]) * (rq * (HEAD_DIM ** -0.5 * LOG2E))).astype(BF16)
    zeros = jnp.zeros((HEAD_DIM, tm), BF16)
    group = GQA_HEADS // GQA_KV_HEADS
    for h in range(GQA_HEADS):
        base = 1024 + LANE * h
        for kv in range(GQA_KV_HEADS):
            qt_ref[base + HEAD_DIM * kv:base + HEAD_DIM * (kv + 1), :] = qb[h] if kv == h // group else zeros
    bk = y1[:, 640:768]
    lane = lax.broadcasted_iota(jnp.int32, bk.shape, 1)
    sq = bk * bk
    ms0 = jnp.sum(jnp.where(lane < HEAD_DIM, sq, 0.0), axis=-1, keepdims=True) * (1.0 / HEAD_DIM)
    ms1 = jnp.sum(jnp.where(lane >= HEAD_DIM, sq, 0.0), axis=-1, keepdims=True) * (1.0 / HEAD_DIM)
    rk = jnp.where(lane < HEAD_DIM, lax.rsqrt(ms0 + EPS), lax.rsqrt(ms1 + EPS))
    kn_ref[:, 1024:1152] = ((bk * tn[:, 256:384] + y1[:, 768:896] * tn[:, 384:512]) * rk).astype(BF16)
    vt_ref[0, 512:640, :] = y2[1024:1152].astype(BF16)


def _ab_proj(xa, mod_l, g, w, tt, tn, *, n_lat_tiles):
    s_pad, d = xa.shape
    n_tiles = s_pad // TM
    mod_spec = pl.BlockSpec((1, N_MOD, d), lambda i: (jnp.where(i >= n_lat_tiles, 1, 0), 0, 0))
    consts = [w["w1"], w["w2t"], w["qn"], w["kvn"], w["wqt"], w["wuk"], w["wuvt"]]
    return pl.pallas_call(
        _ab_proj_kernel,
        grid=(n_tiles,),
        in_specs=[pl.BlockSpec((TM, d), lambda i: (i, 0)), mod_spec, _const_spec((6, d))]
        + [_const_spec(a.shape) for a in consts]
        + [pl.BlockSpec((tt.shape[0], TM), lambda i: (0, i)), pl.BlockSpec((TM, tn.shape[1]), lambda i: (i, 0))],
        out_specs=[pl.BlockSpec((N_HEADS * LANE, TM), lambda i: (0, i)),
                   pl.BlockSpec((TM, 9 * LANE), lambda i: (i, 0)),
                   pl.BlockSpec((1, 10 * HEAD_DIM, TM), lambda i: (i, 0, 0))],
        out_shape=[jax.ShapeDtypeStruct((N_HEADS * LANE, s_pad), BF16),
                   jax.ShapeDtypeStruct((s_pad, 9 * LANE), BF16),
                   jax.ShapeDtypeStruct((n_tiles, 10 * HEAD_DIM, TM), BF16)],
        compiler_params=_params(1),
        name="ab_proj",
    )(xa, mod_l, g, *consts, tt, tn)


def _ab_weights(w_in, q_norm, w_uq, kv_norm, w_ukv):
    d = w_in.shape[0]
    o = 0
    parts = []
    for width in (MLA_Q_RANK, MLA_KV_RANK, MLA_ROPE, GQA_HEADS * HEAD_DIM, GQA_KV_HEADS * HEAD_DIM,
                  GQA_KV_HEADS * HEAD_DIM):
        parts.append(w_in[:, o:o + width])
        o += width
    w_cq, w_ckv, w_kr, w_bq, w_bk, w_bv = parts

    def rope_lanes(w):
        return jnp.zeros((w.shape[0], LANE), F32).at[:, MLA_NOPE:MLA_NOPE + MLA_ROPE].set(w)

    w1 = jnp.concatenate([w_cq, w_ckv, rope_lanes(w_kr), rope_lanes(_swap_halves(w_kr, 1, MLA_ROPE)),
                          w_bk, _swap_halves(w_bk, GQA_KV_HEADS, HEAD_DIM)], axis=1)
    w2t = jnp.concatenate([w_bq, _swap_halves(w_bq, GQA_HEADS, HEAD_DIM), w_bv], axis=1).T
    uq = w_uq.reshape(MLA_Q_RANK, MLA_HEADS, MLA_NOPE + MLA_ROPE)
    wa = jnp.zeros((MLA_Q_RANK, MLA_HEADS, LANE), F32).at[:, :, :MLA_NOPE + MLA_ROPE].set(uq)
    wr = jnp.zeros((MLA_Q_RANK, MLA_HEADS, LANE), F32).at[:, :, MLA_NOPE:MLA_NOPE + MLA_ROPE].set(
        jnp.roll(uq[:, :, MLA_NOPE:], MLA_ROPE // 2, axis=-1))
    wqt = jnp.concatenate([wa.reshape(MLA_Q_RANK, -1), wr.reshape(MLA_Q_RANK, -1)], axis=1).T
    ukv = w_ukv.reshape(MLA_KV_RANK, MLA_HEADS, MLA_NOPE + MLA_V)
    wuk = jnp.zeros((MLA_KV_RANK, MLA_HEADS, LANE), F32).at[:, :, :MLA_NOPE].set(ukv[:, :, :MLA_NOPE])
    wuvt = ukv[:, :, MLA_NOPE:].reshape(MLA_KV_RANK, MLA_HEADS * MLA_V).T
    return dict(w1=w1.astype(BF16), w2t=w2t.astype(BF16), qn=q_norm[None, :], kvn=kv_norm[None, :],
                wqt=wqt.astype(BF16), wuk=wuk.reshape(MLA_KV_RANK, -1).astype(BF16), wuvt=wuvt.astype(BF16))


def _ab_tables(rope_a, rope_h, qn_b, kn_b):
    cos_a, sin_a = rope_a
    cos_h, sin_h = rope_h
    n = cos_a.shape[0]
    cos_a_full = jnp.concatenate([jnp.ones((n, MLA_NOPE), F32), cos_a, jnp.ones((n, LANE - 96), F32)], axis=1)
    sin_a_full = jnp.concatenate([jnp.zeros((n, MLA_NOPE), F32), sin_a, jnp.zeros((n, LANE - 96), F32)], axis=1)
    half = HEAD_DIM // 2
    cgq, sgq = cos_h * qn_b[None, :], sin_h * jnp.roll(qn_b, half)[None, :]
    cgk, sgk = cos_h * kn_b[None, :], sin_h * jnp.roll(kn_b, half)[None, :]
    tt = jnp.concatenate([cos_a_full, sin_a_full, cgq, sgq], axis=1).T
    tn = jnp.concatenate([cos_a_full, sin_a_full, cgk, cgk, sgk, sgk], axis=1)
    return tt, tn


ACC_ROWS = HEAD_DIM + 16


def _flash_kernel(*refs, n_chunks, s_lat, ctx_len, use_sink, aliased):
    refs = list(refs)
    sink_ref = refs.pop(0) if use_sink else None
    qt_ref, k_ref, vt_ref = refs[:3]
    o_ref = refs[4] if aliased else refs[3]
    s_scr, p_scr, m_scr, acc_scr = refs[-4:]
    qt = qt_ref[...]
    tq = qt.shape[1]
    ones = jnp.ones((ACC_ROWS - HEAD_DIM, TM), BF16)

    def scores(c):
        return _dot(k_ref[pl.ds(pl.multiple_of(c * TM, TM), TM), :], qt)

    def pv(vt_blk, p):
        return _dot(jnp.concatenate([vt_blk, ones[:, :vt_blk.shape[1]]], axis=0), p)

    def softmax(s):
        m_old = m_scr[...]
        m_new = jnp.maximum(m_old, jnp.max(s, axis=0, keepdims=True))
        m_scr[...] = m_new
        return jnp.exp2(s - m_new).astype(BF16), jnp.exp2(m_old - m_new)

    m_scr[...] = jnp.full((1, tq), NEG, F32)
    acc_scr[...] = jnp.zeros((ACC_ROWS, tq), F32)
    if n_chunks:
        s_scr[...] = scores(0)
        p_scr[...] = jnp.zeros((TM, tq), BF16)

        def body(j, carry):
            c = 2 * j
            s_next = scores(c + 1)
            acc = acc_scr[...] + pv(vt_ref[jnp.maximum(c - 1, 0)], p_scr[...])
            p, alpha = softmax(s_scr[...])
            acc_scr[...] = acc * alpha
            s_scr[...] = scores(jnp.minimum(c + 2, n_chunks - 1))
            acc = acc_scr[...] + pv(vt_ref[c], p)
            p, alpha = softmax(s_next)
            p_scr[...] = p
            acc_scr[...] = acc * alpha
            return carry

        lax.fori_loop(0, n_chunks // 2, body, 0)
        acc_scr[...] = acc_scr[...] + pv(vt_ref[n_chunks - 1], p_scr[...])
    p, alpha = softmax(_dot(k_ref[s_lat:s_lat + ctx_len, :], qt))
    acc = acc_scr[...] * alpha + pv(vt_ref[s_lat // TM][:, 0:ctx_len], p)
    l = acc[HEAD_DIM:HEAD_DIM + 1]
    if use_sink:
        sink = sink_ref[pl.program_id(0)]
        m = m_scr[...]
        m_fin = jnp.maximum(m, sink)
        alpha = jnp.exp2(m - m_fin)
        l = alpha * l + jnp.exp2(sink - m_fin)
        acc = acc * alpha
    o_ref[...] = (acc[0:HEAD_DIM] / l).astype(BF16)


def _flash(qt, kn, vt, kb, vb, *, s_lat, ctx_len, q_tiles, n_chunks, q_off=0, sink=None, o_prev=None):
    s_pad = qt.shape[1]
    assert n_chunks % 2 == 0
    kern = functools.partial(_flash_kernel, n_chunks=n_chunks, s_lat=s_lat, ctx_len=ctx_len,
                             use_sink=sink is not None, aliased=o_prev is not None)
    in_specs = [pl.BlockSpec((LANE, TM), lambda h, i: (h, i + q_off)),
                pl.BlockSpec((s_pad, LANE), lambda h, i: (0, kb(h))),
                pl.BlockSpec((s_pad // TM, HEAD_DIM, TM), lambda h, i: (0, vb(h), 0))]
    args = [qt, kn, vt]
    aliases = {}
    if o_prev is not None:
        in_specs = in_specs + [pl.BlockSpec(memory_space=pl.ANY)]
        args = args + [o_prev]
        aliases = {3: 0}
    if sink is not None:
        in_specs = [pl.BlockSpec(memory_space=pltpu.SMEM)] + in_specs
        args = [sink] + args
        aliases = {k + 1: v for k, v in aliases.items()}
    return pl.pallas_call(
        kern,
        grid=(N_HEADS, q_tiles),
        in_specs=in_specs,
        out_specs=pl.BlockSpec((HEAD_DIM, TM), lambda h, i: (h, i + q_off)),
        out_shape=jax.ShapeDtypeStruct((N_HEADS * HEAD_DIM, s_pad), BF16),
        scratch_shapes=[pltpu.VMEM((TM, TM), F32), pltpu.VMEM((TM, TM), BF16),
                        pltpu.VMEM((1, TM), F32), pltpu.VMEM((ACC_ROWS, TM), F32)],
        input_output_aliases=aliases,
        compiler_params=_params(2),
        name="flash" if n_chunks else "flash_ctx",
    )(*args)


def _cd_proj_kernel(x_ref, mod_ref, g_ref, w1_ref, w2t_ref, tt_ref, tn_ref, qt_ref, kn_ref, vt_ref):
    x = x_ref[...]
    m = mod_ref[0]
    g = g_ref[...]
    tm = x.shape[0]
    hb = (_rms_rows(x) * (g[2:3] * (1.0 + m[4:5])) + m[3:4]).astype(BF16)
    y1 = _dot(hb, w1_ref[...])
    y2 = _dot_nt(w2t_ref[...], hb)
    tt = tt_ref[...]
    tn = tn_ref[...]
    scale = HEAD_DIM ** -0.5 * LOG2E
    qc = (y2[0:1024].reshape(SWA_HEADS, LANE, tm) * tt[0:128][None]
          + y2[2048:3072].reshape(SWA_HEADS, LANE, tm) * tt[128:256][None])
    qt_ref[0:1024, :] = (qc * scale).reshape(1024, tm).astype(BF16)
    qt_ref[1024:2048, :] = (y2[1024:2048] * scale).astype(BF16)
    kn_ref[:, 0:128] = (y1[:, 0:128] * tn[:, 0:128] + y1[:, 128:256] * tn[:, 128:256]).astype(BF16)
    kn_ref[:, 128:640] = y1[:, 256:768].astype(BF16)
    vt_ref[0] = y2[3072:3712].astype(BF16)


def _cd_proj(xa, mod_l, g, w, tt, tn, *, n_lat_tiles):
    s_pad, d = xa.shape
    n_tiles = s_pad // TM
    mod_spec = pl.BlockSpec((1, N_MOD, d), lambda i: (jnp.where(i >= n_lat_tiles, 1, 0), 0, 0))
    return pl.pallas_call(
        _cd_proj_kernel,
        grid=(n_tiles,),
        in_specs=[pl.BlockSpec((TM, d), lambda i: (i, 0)), mod_spec, _const_spec((6, d)),
                  _const_spec(w["w1"].shape), _const_spec(w["w2t"].shape),
                  pl.BlockSpec((tt.shape[0], TM), lambda i: (0, i)), pl.BlockSpec((TM, tn.shape[1]), lambda i: (i, 0))],
        out_specs=[pl.BlockSpec((N_HEADS * LANE, TM), lambda i: (0, i)),
                   pl.BlockSpec((TM, 5 * LANE), lambda i: (i, 0)),
                   pl.BlockSpec((1, 10 * HEAD_DIM, TM), lambda i: (i, 0, 0))],
        out_shape=[jax.ShapeDtypeStruct((N_HEADS * LANE, s_pad), BF16),
                   jax.ShapeDtypeStruct((s_pad, 5 * LANE), BF16),
                   jax.ShapeDtypeStruct((n_tiles, 10 * HEAD_DIM, TM), BF16)],
        compiler_params=_params(1),
        name="cd_proj",
    )(xa, mod_l, g, w["w1"], w["w2t"], tt, tn)


def _cd_weights(w_in):
    d = w_in.shape[0]
    o = 0
    parts = []
    for width in (SWA_HEADS * HEAD_DIM, SWA_KV_HEADS * HEAD_DIM, SWA_KV_HEADS * HEAD_DIM,
                  NA_HEADS * HEAD_DIM, NA_HEADS * HEAD_DIM, NA_HEADS * HEAD_DIM):
        parts.append(w_in[:, o:o + width])
        o += width
    w_cq, w_ck, w_cv, w_dq, w_dk, w_dv = parts

    def pad_heads(w, n_heads, slot_of_head):
        w3 = w.reshape(d, n_heads, HEAD_DIM)
        out = jnp.zeros((d, n_heads, 2, HEAD_DIM), F32)
        for h in range(n_heads):
            out = out.at[:, h, slot_of_head(h)].set(w3[:, h])
        return out.reshape(d, n_heads * LANE)

    group = SWA_HEADS // SWA_KV_HEADS
    q_c = pad_heads(w_cq, SWA_HEADS, lambda h: h // group)
    q_c_rot = pad_heads(_swap_halves(w_cq, SWA_HEADS, HEAD_DIM), SWA_HEADS, lambda h: h // group)
    q_d = pad_heads(w_dq, NA_HEADS, lambda h: h % 2)
    w1 = jnp.concatenate([w_ck, _swap_halves(w_ck, SWA_KV_HEADS, HEAD_DIM), w_dk], axis=1)
    w2t = jnp.concatenate([q_c, q_d, q_c_rot, w_cv, w_dv], axis=1).T
    return dict(w1=w1.astype(BF16), w2t=w2t.astype(BF16))


def _cd_tables(rope_h):
    cos_h, sin_h = rope_h
    tn = jnp.concatenate([cos_h, cos_h, sin_h, sin_h], axis=1)
    return tn.T, tn


def _na_table_kernel(rpb_ref, o_ref, p_scr, *, rows, n_tiles):
    t = pl.program_id(0)
    h = pl.program_id(1)
    rows_per_tile = TQ_WIN // GRID_W
    n_rel_r, n_rel_c = 2 * NA_ROWS - 1, 2 * NA_COLS - 1
    tile = jnp.where(t == 0, 0, jnp.where(t == 1, 1, n_tiles - 1))
    start = jnp.clip(tile - 1, 0, n_tiles - 3)
    shape = (GRID_W, TQ_WIN)
    kc = lax.broadcasted_iota(jnp.int32, shape, 0)
    ql = lax.broadcasted_iota(jnp.int32, shape, 1)
    col = ql % GRID_W
    r = tile * rows_per_tile + ql // GRID_W
    c0 = jnp.clip(col - NA_COLS // 2, 0, GRID_W - NA_COLS)
    rel_c = kc - col + NA_COLS - 1
    r0 = jnp.clip(r - NA_ROWS // 2, 0, rows - NA_ROWS)

    def build_row(i, carry):
        base = (h * n_rel_r + i) * n_rel_c
        acc = jnp.zeros(shape, F32)
        for j in range(n_rel_c):
            acc = jnp.where(rel_c == j, rpb_ref[base + j], acc)
        p_scr[i] = jnp.where(kc >= c0, jnp.where(kc < c0 + NA_COLS, acc * LOG2E, NEG), NEG)
        return carry

    lax.fori_loop(0, n_rel_r, build_row, 0)
    for b in range(3 * rows_per_tile):
        kr = start * rows_per_tile + b
        rel_r = jnp.where(kr >= r0, jnp.where(kr < r0 + NA_ROWS, kr - r + NA_ROWS - 1, -1), -1)
        blk = jnp.full(shape, NEG, F32)
        for i in range(n_rel_r):
            blk = jnp.where(rel_r == i, p_scr[i], blk)
        o_ref[0, 0, b * GRID_W:(b + 1) * GRID_W, :] = blk


def _na_bias_tables(rpb, rows, n_tiles):
    n_heads = rpb.shape[0]
    return pl.pallas_call(
        functools.partial(_na_table_kernel, rows=rows, n_tiles=n_tiles),
        grid=(3, n_heads),
        in_specs=[pl.BlockSpec(memory_space=pltpu.SMEM)],
        out_specs=pl.BlockSpec((1, 1, 3 * TQ_WIN, TQ_WIN), lambda t, h: (t, h, 0, 0)),
        out_shape=jax.ShapeDtypeStruct((3, n_heads, 3 * TQ_WIN, TQ_WIN), F32),
        scratch_shapes=[pltpu.VMEM((2 * NA_ROWS - 1, GRID_W, TQ_WIN), F32)],
        compiler_params=_params(2),
        name="na_table",
    )(rpb.reshape(-1))


def _cd_attn_kernel(sink_ref, qt_ref, k0_ref, k1_ref, k2_ref, kc_ref, v0_ref, v1_ref, v2_ref, vc_ref, tab_ref,
                    o_ref, *, n_tiles):
    h = pl.program_id(0)
    i = pl.program_id(1)
    start = jnp.clip(i - 1, 0, n_tiles - 3)
    qt = qt_ref[...]
    kw = jnp.concatenate([k0_ref[...], k1_ref[...], k2_ref[...]], axis=0)
    sw = _dot(kw, qt)
    sc = _dot(kc_ref[...], qt)
    kpos = start * TQ_WIN + lax.broadcasted_iota(jnp.int32, sw.shape, 0)
    qpos = i * TQ_WIN + lax.broadcasted_iota(jnp.int32, sw.shape, 1)
    band = jnp.where(jnp.abs(qpos - kpos) <= WINDOW, 0.0, NEG)
    sw = sw + jnp.where(h < SWA_HEADS, band, tab_ref[0, 0])
    sink = sink_ref[h]
    m = jnp.maximum(jnp.maximum(jnp.max(sw, axis=0, keepdims=True), jnp.max(sc, axis=0, keepdims=True)), sink)
    pw = jnp.exp2(sw - m)
    pc = jnp.exp2(sc - m)
    l = jnp.sum(pw, axis=0, keepdims=True) + jnp.sum(pc, axis=0, keepdims=True) + jnp.exp2(sink - m)
    vw = jnp.concatenate([v0_ref[0], v1_ref[0], v2_ref[0]], axis=1)
    acc = _dot(vw, pw.astype(BF16)) + _dot(vc_ref[0], pc.astype(BF16))
    o_ref[...] = (acc / l).astype(BF16)


def _cd_attn(qt, kn, vt, tab, sink, kb, vb, *, s_lat, ctx_len):
    s_pad = qt.shape[1]
    n_tiles = s_lat // TQ_WIN
    per_chunk = TM // TQ_WIN

    def start(i):
        return jnp.clip(i - 1, 0, n_tiles - 3)

    def k_spec(j):
        return pl.BlockSpec((TQ_WIN, LANE), lambda h, i: (start(i) + j, kb(h)))

    def v_spec(j):
        return pl.BlockSpec((1, HEAD_DIM, TQ_WIN),
                            lambda h, i: ((start(i) + j) // per_chunk, vb(h), (start(i) + j) % per_chunk))

    def tab_map(h, i):
        return (jnp.where(i == 0, 0, jnp.where(i == n_tiles - 1, 2, 1)), jnp.maximum(h - SWA_HEADS, 0), 0, 0)

    return pl.pallas_call(
        functools.partial(_cd_attn_kernel, n_tiles=n_tiles),
        grid=(N_HEADS, n_tiles),
        in_specs=[pl.BlockSpec(memory_space=pltpu.SMEM),
                  pl.BlockSpec((LANE, TQ_WIN), lambda h, i: (h, i)),
                  k_spec(0), k_spec(1), k_spec(2),
                  pl.BlockSpec((ctx_len, LANE), lambda h, i: (s_lat // ctx_len, kb(h))),
                  v_spec(0), v_spec(1), v_spec(2),
                  pl.BlockSpec((1, HEAD_DIM, ctx_len), lambda h, i: (s_lat // TM, vb(h), 0)),
                  pl.BlockSpec((1, 1, 3 * TQ_WIN, TQ_WIN), tab_map)],
        out_specs=pl.BlockSpec((HEAD_DIM, TQ_WIN), lambda h, i: (h, i)),
        out_shape=jax.ShapeDtypeStruct((N_HEADS * HEAD_DIM, s_pad), BF16),
        compiler_params=_params(2),
        name="cd_attn",
    )(sink, qt, kn, kn, kn, kn, vt, vt, vt, vt, tab)


def _ab_kb(h):
    return jnp.minimum(h, MLA_HEADS)


def _ab_vb(h):
    return jnp.where(h < MLA_HEADS, h, MLA_HEADS + (h - MLA_HEADS) // (GQA_HEADS // GQA_KV_HEADS))


def _cd_kb(h):
    return jnp.where(h < SWA_HEADS, 0, 1 + (h - SWA_HEADS) // 2)


def _cd_vb(h):
    return jnp.where(h < SWA_HEADS, h // (SWA_HEADS // SWA_KV_HEADS), SWA_KV_HEADS + h - SWA_HEADS)


def kernel(x, c, ctx, c_ctx, ada_w, ada_b, norm_g, ffn_w_gate, ffn_w_up, ffn_w_down, ab_w_in, mla_q_norm, mla_w_uq,
           mla_kv_norm, mla_w_ukv, gqa_q_norm, gqa_k_norm, ab_w_out, cd_w_in, swa_sink, na_rpb, cd_w_out):
    batch, s_lat, d = x.shape
    ctx_len = ctx.shape[1]
    depth = ada_w.shape[0]
    rows = s_lat // GRID_W
    assert batch == 1 and c.shape[0] == 1 and ctx.shape[0] == 1
    assert s_lat % TM == 0 and s_lat % ctx_len == 0 and ctx_len % LANE == 0 and ctx_len <= TM
    assert s_lat // TQ_WIN >= 3 and rows >= NA_ROWS + TQ_WIN // GRID_W
    s_pad = s_lat + TM
    n_lat_tiles = s_lat // TM

    xa = jnp.concatenate([x[0], ctx[0], jnp.zeros((TM - ctx_len, d), F32)], axis=0)
    mods = _modulation(c, c_ctx, ada_w, ada_b)
    rope_a = _stream_rope(s_lat, s_pad, MLA_ROPE)
    rope_h = _stream_rope(s_lat, s_pad, HEAD_DIM)
    cd_tt, cd_tn = _cd_tables(rope_h)

    for l in range(depth):
        last = l == depth - 1
        mod_l, g = mods[l], norm_g[l]
        wg, wu, wd = ffn_w_gate[l].astype(BF16), ffn_w_up[l].astype(BF16), ffn_w_down[l].astype(BF16)
        xa = _ffn(xa, mod_l, g, wg[0], wu[0], wd[0], slot=0, n_lat_tiles=n_lat_tiles)
        j = l // 2
        if l % 2 == 0:
            w = _ab_weights(ab_w_in[j], mla_q_norm[j], mla_w_uq[j], mla_kv_norm[j], mla_w_ukv[j])
            tt, tn = _ab_tables(rope_a, rope_h, gqa_q_norm[j], gqa_k_norm[j])
            qt, kn, vt = _ab_proj(xa, mod_l, g, w, tt, tn, n_lat_tiles=n_lat_tiles)
            ot = _flash(qt, kn, vt, _ab_kb, _ab_vb, s_lat=s_lat, ctx_len=ctx_len, q_tiles=n_lat_tiles,
                        n_chunks=n_lat_tiles)
            if not last:
                ot = _flash(qt, kn, vt, _ab_kb, _ab_vb, s_lat=s_lat, ctx_len=ctx_len, q_tiles=1, n_chunks=0,
                            q_off=n_lat_tiles, o_prev=ot)
            wo = ab_w_out[j].astype(BF16)
        else:
            w = _cd_weights(cd_w_in[j])
            qt, kn, vt = _cd_proj(xa, mod_l, g, w, cd_tt, cd_tn, n_lat_tiles=n_lat_tiles)
            tab = _na_bias_tables(na_rpb[j], rows, s_lat // TQ_WIN)
            sink = jnp.concatenate([swa_sink[j] * LOG2E, jnp.full((NA_HEADS,), NEG, F32)])
            ot = _cd_attn(qt, kn, vt, tab, sink, _cd_kb, _cd_vb, s_lat=s_lat, ctx_len=ctx_len)
            if not last:
                ot = _flash(qt, kn, vt, _cd_kb, _cd_vb, s_lat=s_lat, ctx_len=ctx_len, q_tiles=1, n_chunks=0,
                            q_off=n_lat_tiles, sink=sink, o_prev=ot)
            wo = cd_w_out[j].astype(BF16)
        xa = _ffn(xa, mod_l, g, wg[1], wu[1], wd[1], slot=2, n_lat_tiles=n_lat_tiles, ot=ot, wo=wo,
                  out_rows=s_lat if last else None)
    return xa[None]
```
